```python
import jax, jax.numpy as jnp
from jax import lax
import numpy as np

D_MODEL = 1024
BATCH = 2
SEQ = 8192
DEPTH = 2
DEC_BATCH = 32
DEC_SEQ = 8
PAST_LEN = 16384
PAGE_SIZE = 128

SB_HEADS = 8
HEAD_DIM = 64
SB_WIDTH = SB_HEADS * HEAD_DIM
SB_SCALE = HEAD_DIM ** -0.5
SB_BIAS_INIT = -8.0
Q_BLOCK = 128
POOL_WINDOWS = (2, 4, 8, 16)
POOL_GROUPS = len(POOL_WINDOWS)
POOL_WIDTH = D_MODEL // 2
POOL_GROUP_WIDTH = POOL_WIDTH // POOL_GROUPS
POOL_STATE = max(POOL_WINDOWS) - 1
AB_IN = 3 * SB_WIDTH + POOL_WIDTH
AB_MIX = SB_WIDTH + POOL_WIDTH
C_CHUNK = 128
C_WIDTH = D_MODEL
C_GROUPS = 8
C_GROUP_WIDTH = C_WIDTH // C_GROUPS
D_FF = 2816
N_AB = (DEPTH + 1) // 2
N_C = DEPTH // 2
EPS = 1e-6

kernel_name = 'stickbreak_pool_gmlp_macaron_step'


def rms_norm(x, g):
    xf = x.astype(jnp.float32)
    y = xf * lax.rsqrt(jnp.mean(xf * xf, axis=-1, keepdims=True) + EPS)
    return (y * g).astype(x.dtype)


def layer_norm(x, g, b):
    xf = x.astype(jnp.float32)
    mu = jnp.mean(xf, axis=-1, keepdims=True)
    xc = xf - mu
    y = xc * lax.rsqrt(jnp.mean(xc * xc, axis=-1, keepdims=True) + EPS)
    return (y * g + b).astype(x.dtype)


def swiglu(x, w_gate, w_up, w_down):
    return (jax.nn.silu(x @ w_gate) * (x @ w_up)) @ w_down


def sb_weights(z, mask):
    ax = z.ndim - 1
    log_stay = jnp.where(mask, jax.nn.log_sigmoid(-z), 0.0)
    after = lax.cumsum(log_stay, axis=ax, reverse=True) - log_stay
    return jnp.where(mask, jnp.exp(jax.nn.log_sigmoid(z) + after), 0.0)


def sb_attend_prompt(q, k, v, bias):
    b, s = q.shape[0], q.shape[1]
    nb = s // Q_BLOCK
    qb = q.reshape(b, nb, Q_BLOCK, SB_HEADS, HEAD_DIM).swapaxes(0, 1)
    k_pos = jnp.arange(s)
    bias4 = bias.astype(jnp.float32)[None, :, None, None]

    def block(args):
        q_blk, i = args
        q_pos = i * Q_BLOCK + jnp.arange(Q_BLOCK)
        z = jnp.einsum('bqhd,bkhd->bhqk', q_blk, k, preferred_element_type=jnp.float32) * SB_SCALE + bias4
        a = sb_weights(z, k_pos[None, :] < q_pos[:, None])
        return jnp.einsum('bhqk,bkhd->bqhd', a.astype(v.dtype), v)

    o = lax.map(block, (qb, jnp.arange(nb)))
    return o.swapaxes(0, 1).reshape(b, s, SB_WIDTH)


def sb_attend_sample(q, k_new, v_new, k_past, v_past, bias):
    n, t = q.shape[0], q.shape[1]
    p = k_past.shape[1]
    z_past = jnp.einsum('nthd,nphd->nhtp', q, k_past, preferred_element_type=jnp.float32)
    z_new = jnp.einsum('nthd,nshd->nhts', q, k_new, preferred_element_type=jnp.float32)
    z = (jnp.concatenate([z_past, z_new], axis=-1) * SB_SCALE
         + bias.astype(jnp.float32)[None, :, None, None])
    q_pos = p + jnp.arange(t)
    k_pos = jnp.arange(p + t)
    a = sb_weights(z, k_pos[None, :] < q_pos[:, None]).astype(v_new.dtype)
    o = (jnp.einsum('nhtp,nphd->nthd', a[..., :p], v_past)
         + jnp.einsum('nhts,nshd->nthd', a[..., p:], v_new))
    return o.reshape(n, t, SB_WIDTH)


def ab_project(h, w_in, q_gain, k_gain):
    n, t = h.shape[0], h.shape[1]
    pr = h @ w_in
    q = rms_norm(pr[..., :SB_WIDTH].reshape(n, t, SB_HEADS, HEAD_DIM), q_gain)
    k = rms_norm(pr[..., SB_WIDTH:2 * SB_WIDTH].reshape(n, t, SB_HEADS, HEAD_DIM), k_gain)
    v = pr[..., 2 * SB_WIDTH:3 * SB_WIDTH].reshape(n, t, SB_HEADS, HEAD_DIM)
    u = pr[..., 3 * SB_WIDTH:]
    return q, k, v, u


def pool_mix(u, prev, first_pos, w_pool, pool_scale):
    n, t = u.shape[0], u.shape[1]
    ext = jnp.concatenate([prev.astype(u.dtype), u], axis=1)
    ext32 = ext.astype(jnp.float32)
    cs = jnp.concatenate([jnp.zeros((n, 1, POOL_WIDTH), jnp.float32), jnp.cumsum(ext32, axis=1)], axis=1)
    cur = ext32[:, POOL_STATE:]
    pos = first_pos + jnp.arange(t)
    end = POOL_STATE + 1
    outs = []
    for g, w in enumerate(POOL_WINDOWS):
        lo, hi = g * POOL_GROUP_WIDTH, (g + 1) * POOL_GROUP_WIDTH
        wsum = cs[:, end:end + t, lo:hi] - cs[:, end - w:end - w + t, lo:hi]
        cnt = jnp.minimum(pos + 1, w).astype(jnp.float32)
        d = (wsum / cnt[None, :, None] - cur[..., lo:hi]).astype(u.dtype)
        outs.append(d @ w_pool[g])
    y = jnp.concatenate(outs, axis=-1) * pool_scale
    return y, ext[:, -POOL_STATE:]


def c_project(h, w_in, ln_g, ln_b):
    z = jax.nn.gelu(h @ w_in)
    u = z[..., :C_WIDTH]
    vn = layer_norm(z[..., C_WIDTH:], ln_g, ln_b)
    return u, vn


def spatial_gate(vn, w_s, b_s):
    n, t = vn.shape[0], vn.shape[1]
    pad = (-t) % C_CHUNK
    vpad = jnp.pad(vn, ((0, 0), (0, pad), (0, 0)))
    nc = (t + pad) // C_CHUNK
    vr = vpad.reshape(n, nc, C_CHUNK, C_GROUPS, C_GROUP_WIDTH)
    causal = jnp.tril(jnp.ones((C_CHUNK, C_CHUNK), dtype=bool))
    ws = jnp.where(causal, w_s, 0).astype(vn.dtype)
    out = jnp.einsum('gts,ncsgd->nctgd', ws, vr) + b_s.T[None, None, :, :, None].astype(vn.dtype)
    return out.reshape(n, nc * C_CHUNK, C_WIDTH)[:, :t]


def setup_inputs(seed: int = 0) -> dict:
    key = jax.random.key(seed)
    ks = jax.random.split(key, 32)
    n_pages = PAST_LEN // PAGE_SIZE
    n_used = DEC_BATCH * n_pages
    n_phys = n_used + (n_used + 3) // 4
    f32 = jnp.float32

    def nrm(k, shape, scale=1.0):
        return jax.random.normal(k, shape, f32) * scale

    def gain(k, shape):
        return 1.0 + 0.05 * jax.random.normal(k, shape, f32)

    page_table = jax.random.permutation(ks[5], n_phys)[:n_used].reshape(DEC_BATCH, n_pages).astype(jnp.int32)
    return {
        'x_prompt': nrm(ks[0], (BATCH, SEQ, D_MODEL)),
        'x_sample': nrm(ks[1], (DEC_BATCH, DEC_SEQ, D_MODEL)),
        'cache_k': nrm(ks[2], (N_AB, n_phys, PAGE_SIZE, SB_HEADS, HEAD_DIM)),
        'cache_v': nrm(ks[3], (N_AB, n_phys, PAGE_SIZE, SB_HEADS, HEAD_DIM)),
        'state_pool': nrm(ks[4], (N_AB, DEC_BATCH, POOL_STATE, POOL_WIDTH)),
        'page_table': page_table,
        'norm_ffn1': gain(ks[6], (DEPTH, D_MODEL)),
        'norm_mix': gain(ks[7], (DEPTH, D_MODEL)),
        'norm_ffn2': gain(ks[8], (DEPTH, D_MODEL)),
        'ffn1_gate': nrm(ks[9], (DEPTH, D_MODEL, D_FF), D_MODEL ** -0.5),
        'ffn1_up': nrm(ks[10], (DEPTH, D_MODEL, D_FF), D_MODEL ** -0.5),
        'ffn1_down': nrm(ks[11], (DEPTH, D_FF, D_MODEL), D_FF ** -0.5),
        'ffn2_gate': nrm(ks[12], (DEPTH, D_MODEL, D_FF), D_MODEL ** -0.5),
        'ffn2_up': nrm(ks[13], (DEPTH, D_MODEL, D_FF), D_MODEL ** -0.5),
        'ffn2_down': nrm(ks[14], (DEPTH, D_FF, D_MODEL), D_FF ** -0.5),
        'ab_w_in': nrm(ks[15], (N_AB, D_MODEL, AB_IN), D_MODEL ** -0.5),
        'ab_q_norm': gain(ks[16], (N_AB, HEAD_DIM)),
        'ab_k_norm': gain(ks[17], (N_AB, HEAD_DIM)),
        'ab_sb_bias': SB_BIAS_INIT + 0.5 * jax.random.normal(ks[27], (N_AB, SB_HEADS), f32),
        'ab_w_pool': nrm(ks[18], (N_AB, POOL_GROUPS, POOL_GROUP_WIDTH, POOL_GROUP_WIDTH), POOL_GROUP_WIDTH ** -0.5),
        'ab_pool_scale': gain(ks[19], (N_AB, POOL_WIDTH)),
        'ab_w_out': nrm(ks[20], (N_AB, AB_MIX, D_MODEL), AB_MIX ** -0.5),
        'c_w_in': nrm(ks[21], (N_C, D_MODEL, 2 * C_WIDTH), D_MODEL ** -0.5),
        'c_ln_g': gain(ks[22], (N_C, C_WIDTH)),
        'c_ln_b': nrm(ks[23], (N_C, C_WIDTH), 0.02),
        'c_w_spatial': nrm(ks[24], (N_C, C_GROUPS, C_CHUNK, C_CHUNK), C_CHUNK ** -0.5),
        'c_b_spatial': gain(ks[25], (N_C, C_GROUPS, C_CHUNK)),
        'c_w_out': nrm(ks[26], (N_C, C_WIDTH, D_MODEL), C_WIDTH ** -0.5),
    }


def reference(x_prompt, x_sample, cache_k, cache_v, state_pool, page_table,
              norm_ffn1, norm_mix, norm_ffn2,
              ffn1_gate, ffn1_up, ffn1_down, ffn2_gate, ffn2_up, ffn2_down,
              ab_w_in, ab_q_norm, ab_k_norm, ab_sb_bias, ab_w_pool, ab_pool_scale, ab_w_out,
              c_w_in, c_ln_g, c_ln_b, c_w_spatial, c_b_spatial, c_w_out):
    dec_b = x_sample.shape[0]
    past_len = page_table.shape[1] * cache_k.shape[2]
    xp, xs = x_prompt, x_sample
    kp_l, vp_l, pp_l, ks_l, vs_l, ps_l, cv_l = [], [], [], [], [], [], []
    for layer in range(DEPTH):
        f1 = (ffn1_gate[layer], ffn1_up[layer], ffn1_down[layer])
        f2 = (ffn2_gate[layer], ffn2_up[layer], ffn2_down[layer])
        xp = xp + 0.5 * swiglu(rms_norm(xp, norm_ffn1[layer]), *f1)
        xs = xs + 0.5 * swiglu(rms_norm(xs, norm_ffn1[layer]), *f1)
        hp = rms_norm(xp, norm_mix[layer])
        hs = rms_norm(xs, norm_mix[layer])
        if layer % 2 == 0:
            a = layer // 2
            qp, kp, vp, up = ab_project(hp, ab_w_in[a], ab_q_norm[a], ab_k_norm[a])
            att_p = sb_attend_prompt(qp, kp, vp, ab_sb_bias[a])
            zero_prev = jnp.zeros((up.shape[0], POOL_STATE, POOL_WIDTH), up.dtype)
            pool_p, pst_p = pool_mix(up, zero_prev, 0, ab_w_pool[a], ab_pool_scale[a])
            xp = xp + jnp.concatenate([att_p, pool_p], axis=-1) @ ab_w_out[a]
            qs, ks_, vs_, us = ab_project(hs, ab_w_in[a], ab_q_norm[a], ab_k_norm[a])
            k_past = cache_k[a][page_table].reshape(dec_b, past_len, SB_HEADS, HEAD_DIM)
            v_past = cache_v[a][page_table].reshape(dec_b, past_len, SB_HEADS, HEAD_DIM)
            att_s = sb_attend_sample(qs, ks_, vs_, k_past, v_past, ab_sb_bias[a])
            pool_s, pst_s = pool_mix(us, state_pool[a], past_len, ab_w_pool[a], ab_pool_scale[a])
            xs = xs + jnp.concatenate([att_s, pool_s], axis=-1) @ ab_w_out[a]
            kp_l.append(kp); vp_l.append(vp); pp_l.append(pst_p)
            ks_l.append(ks_); vs_l.append(vs_); ps_l.append(pst_s)
        else:
            c = layer // 2
            up, vnp = c_project(hp, c_w_in[c], c_ln_g[c], c_ln_b[c])
            xp = xp + (up * spatial_gate(vnp, c_w_spatial[c], c_b_spatial[c])) @ c_w_out[c]
            us, vns = c_project(hs, c_w_in[c], c_ln_g[c], c_ln_b[c])
            xs = xs + (us * spatial_gate(vns, c_w_spatial[c], c_b_spatial[c])) @ c_w_out[c]
            cv_l.append(vns)
        xp = xp + 0.5 * swiglu(rms_norm(xp, norm_ffn2[layer]), *f2)
        xs = xs + 0.5 * swiglu(rms_norm(xs, norm_ffn2[layer]), *f2)
    return (xp, xs, jnp.stack(kp_l), jnp.stack(vp_l), jnp.stack(pp_l),
            jnp.stack(ks_l), jnp.stack(vs_l), jnp.stack(ps_l), jnp.stack(cv_l))
```

```python
import functools
import math

import jax
import jax.numpy as jnp
from jax import lax
from jax.experimental import pallas as pl
from jax.experimental.pallas import tpu as pltpu

F32 = jnp.float32
BF16 = jnp.bfloat16

EPS = 1e-6
POOL_WINDOWS = (2, 4, 8, 16)
POOL_HALO = 16
GELU_C = math.sqrt(2.0 / math.pi)

V7X_VMEM_LIMIT_BYTES = 56 * 1024 * 1024
ATTN_TK = 256
ATTN_TQ = 256
SAMPLE_PAGES_PER_STEP = 8


def _params(*sem):
    return pltpu.CompilerParams(dimension_semantics=sem, vmem_limit_bytes=V7X_VMEM_LIMIT_BYTES)


def _resident(shape):
    nd = len(shape)
    return pl.BlockSpec(shape, lambda *_: (0,) * nd, pipeline_mode=pl.Buffered(1))


def _rms(x, g):
    return x * lax.rsqrt(jnp.mean(x * x, axis=-1, keepdims=True) + EPS) * g


def _softplus_parts(z):
    sp = jnp.maximum(z, 0.0) + jnp.log(1.0 + jnp.exp(-jnp.abs(z)))
    return sp, z - sp


def _ffn_kernel(x_ref, g_ref, wg_ref, wu_ref, wd_ref, o_ref):
    x = x_ref[...]
    h = _rms(x, g_ref[...]).astype(BF16)
    gate = jnp.dot(h, wg_ref[...], preferred_element_type=F32)
    up = jnp.dot(h, wu_ref[...], preferred_element_type=F32)
    act = (gate * jax.nn.sigmoid(gate) * up).astype(BF16)
    o_ref[...] = x + 0.5 * jnp.dot(act, wd_ref[...], preferred_element_type=F32)


def _ffn(x, g, wg, wu, wd, tm):
    t, d = x.shape
    dff = wg.shape[1]
    return pl.pallas_call(
        _ffn_kernel,
        out_shape=jax.ShapeDtypeStruct((t, d), F32),
        grid=(t // tm,),
        in_specs=[
            pl.BlockSpec((tm, d), lambda i: (i, 0)),
            _resident((1, d)),
            _resident((d, dff)),
            _resident((d, dff)),
            _resident((dff, d)),
        ],
        out_specs=pl.BlockSpec((tm, d), lambda i: (i, 0)),
        compiler_params=_params("parallel"),
        name="ffn",
    )(x, g, wg, wu, wd)


def _ab_project_kernel(x_ref, g_ref, wn_ref, wt_ref, hsum_ref, gk_ref, gq_ref,
                       k_ref, v_ref, u_ref, kb_ref, qt_ref, vt_ref, *, width, head_dim, tk):
    h = _rms(x_ref[...], g_ref[...]).astype(BF16)
    nat = jnp.dot(h, wn_ref[...], preferred_element_type=F32)
    k = nat[:, :width]
    ksq = k * k
    ksq_hi = ksq.astype(BF16)
    ksq_lo = (ksq - ksq_hi.astype(F32)).astype(BF16)
    ss = (jnp.dot(ksq_hi, hsum_ref[...], preferred_element_type=F32)
          + jnp.dot(ksq_lo, hsum_ref[...], preferred_element_type=F32))
    kn = k * lax.rsqrt(ss * (1.0 / head_dim) + EPS) * gk_ref[...]
    k_ref[...] = kn
    kb_ref[...] = kn.astype(BF16)
    v_ref[...] = nat[:, width:2 * width]
    u_ref[...] = nat[:, 2 * width:]
    tr = lax.dot_general(wt_ref[...], h, (((1,), (1,)), ((), ())), preferred_element_type=F32)
    gq = gq_ref[...]
    tm = tr.shape[1]
    for hd in range(width // head_dim):
        r0 = hd * head_dim
        q = tr[r0:r0 + head_dim, :]
        ms = jnp.mean(q * q, axis=0, keepdims=True)
        qt_ref[r0:r0 + head_dim, :] = (q * lax.rsqrt(ms + EPS) * gq[r0:r0 + head_dim, :]).astype(BF16)
    vt = tr[width:, :].astype(BF16)
    for c in range(tm // tk):
        vt_ref[c] = vt[:, c * tk:(c + 1) * tk]


def _ab_project(x, g, wn, wt, hsum, gk, gq, head_dim, tm, tk):
    t, d = x.shape
    width = gk.shape[1]
    row = lambda i: (i, 0)
    return pl.pallas_call(
        functools.partial(_ab_project_kernel, width=width, head_dim=head_dim, tk=tk),
        out_shape=(
            jax.ShapeDtypeStruct((t, width), F32),
            jax.ShapeDtypeStruct((t, width), F32),
            jax.ShapeDtypeStruct((t, width), F32),
            jax.ShapeDtypeStruct((t, width), BF16),
            jax.ShapeDtypeStruct((width, t), BF16),
            jax.ShapeDtypeStruct((t // tk, width, tk), BF16),
        ),
        grid=(t // tm,),
        in_specs=[
            pl.BlockSpec((tm, d), row),
            _resident((1, d)),
            _resident(wn.shape),
            _resident(wt.shape),
            _resident(hsum.shape),
            _resident((1, width)),
            _resident((width, 1)),
        ],
        out_specs=(
            pl.BlockSpec((tm, width), row),
            pl.BlockSpec((tm, width), row),
            pl.BlockSpec((tm, width), row),
            pl.BlockSpec((tm, width), row),
            pl.BlockSpec((width, tm), lambda i: (0, i)),
            pl.BlockSpec((tm // tk, width, tk), lambda i: (i, 0, 0)),
        ),
        compiler_params=_params("parallel"),
        name="ab_project",
    )(x, g, wn, wt, hsum, gk, gq)


def _attn_prompt_kernel(bias_ref, qt_ref, k_ref, vt_ref, lmat_ref, o_ref, *, tq, tk, head_dim, scale):
    hd = pl.program_id(1)
    qi = pl.program_id(2)
    bias = bias_ref[hd]
    q = qt_ref[...]
    row = lax.broadcasted_iota(jnp.int32, q.shape, 0)
    own = (hd % 2) * head_dim
    qm = jnp.where((row >= own) & (row < own + head_dim), q, jnp.zeros_like(q))
    lmat = lmat_ref[...]
    ntile = tq // tk

    def tile(j, carry, acc, masked):
        k0 = pl.multiple_of(j * tk, tk)
        s = jnp.dot(k_ref[pl.ds(k0, tk), :], qm, preferred_element_type=F32)
        sp, lsig = _softplus_parts(s * scale + bias)
        if masked:
            kpos = k0 + lax.broadcasted_iota(jnp.int32, (tk, tq), 0)
            qpos = qi * tq + lax.broadcasted_iota(jnp.int32, (tk, tq), 1)
            valid = kpos < qpos
            sp = jnp.where(valid, sp, 0.0)
        spb = sp.astype(BF16)
        suffix = jnp.dot(lmat, spb, preferred_element_type=F32)
        a = jnp.exp(lsig - suffix - carry)
        if masked:
            a = jnp.where(valid, a, 0.0)
        acc = acc + jnp.dot(vt_ref[j], a.astype(BF16), preferred_element_type=F32)
        carry = carry + suffix[0:1, :] + spb[0:1, :].astype(F32)
        return carry, acc

    carry = jnp.zeros((1, tq), F32)
    acc = jnp.zeros((head_dim, tq), F32)
    for d in range(ntile):
        carry, acc = tile(qi * ntile + (ntile - 1 - d), carry, acc, True)

    def body(it, c):
        return tile(qi * ntile - 1 - it, c[0], c[1], False)

    carry, acc = lax.fori_loop(0, qi * ntile, body, (carry, acc))
    o_ref[...] = acc


def _attn_prompt(bias, qt, kb, vt3, lmat, batch, seq, head_dim, tq, tk):
    width = qt.shape[0]
    heads = width // head_dim
    nq = seq // tq
    pair = 2 * head_dim
    grid_spec = pltpu.PrefetchScalarGridSpec(
        num_scalar_prefetch=1,
        grid=(batch, heads, nq),
        in_specs=[
            pl.BlockSpec((pair, tq), lambda b, h, i, _: (h // 2, b * nq + i)),
            pl.BlockSpec((seq, pair), lambda b, h, i, _: (b, h // 2)),
            pl.BlockSpec((seq // tk, head_dim, tk), lambda b, h, i, _: (b, h, 0)),
            pl.BlockSpec((tk, tk), lambda b, h, i, _: (0, 0), pipeline_mode=pl.Buffered(1)),
        ],
        out_specs=pl.BlockSpec((head_dim, tq), lambda b, h, i, _: (h, b * nq + i)),
    )
    return pl.pallas_call(
        functools.partial(_attn_prompt_kernel, tq=tq, tk=tk, head_dim=head_dim, scale=head_dim ** -0.5),
        out_shape=jax.ShapeDtypeStruct((width, batch * seq), F32),
        grid_spec=grid_spec,
        compiler_params=_params("parallel", "parallel", "arbitrary"),
        name="attn_prompt",
    )(bias, qt, kb, vt3, lmat)


def _attn_sample_kernel(pt_ref, qbd_ref, bias_ref, lmat_ref, sel_ref, knew_ref, vnew_ref, *rest,
                        pages, dec_seq, scale):
    k_refs = rest[:pages]
    v_refs = rest[pages:2 * pages]
    o_ref = rest[2 * pages]
    carry_ref, acc_ref = rest[2 * pages + 1:]
    j = pl.program_id(1)
    qbd = qbd_ref[...]
    bias = bias_ref[...]
    lmat = lmat_ref[...]

    def page(kp, vp, masked):
        s = jnp.dot(kp.astype(BF16), qbd, preferred_element_type=F32)
        sp, lsig = _softplus_parts(s * scale + bias)
        if masked:
            key = lax.broadcasted_iota(jnp.int32, s.shape, 0)
            qry = lax.broadcasted_iota(jnp.int32, s.shape, 1) % dec_seq
            valid = key < qry
            sp = jnp.where(valid, sp, 0.0)
        spb = sp.astype(BF16)
        suffix = jnp.dot(lmat, spb, preferred_element_type=F32)
        a = jnp.exp(lsig - suffix - carry_ref[...])
        if masked:
            a = jnp.where(valid, a, 0.0)
        acc_ref[...] += jnp.dot(a.T.astype(BF16), vp.astype(BF16), preferred_element_type=F32)
        carry_ref[...] += suffix[0:1, :] + spb[0:1, :].astype(F32)

    @pl.when(j == 0)
    def _():
        carry_ref[...] = jnp.zeros_like(carry_ref)
        acc_ref[...] = jnp.zeros_like(acc_ref)
        page(knew_ref[...], vnew_ref[...], True)

    for g in range(pages):
        page(k_refs[g][...], v_refs[g][...], False)

    @pl.when(j == pl.num_programs(1) - 1)
    def _():
        own = acc_ref[...] * sel_ref[...]
        heads = own.shape[0] // dec_seq
        out = own[0:dec_seq, :]
        for hd in range(1, heads):
            out = out + own[hd * dec_seq:(hd + 1) * dec_seq, :]
        o_ref[...] = out


def _attn_sample(page_table, qbd, bias_row, lmat, sel, knew, vnew, cache_k, cache_v, layer, dec_seq, head_dim, pages):
    n_seq, n_pages = page_table.shape
    _, _, page_size, width = cache_k.shape
    cols = qbd.shape[2]
    steps = n_pages // pages

    def page_map(g):
        def index(n, j, pt):
            return (layer, pt[n * n_pages + (n_pages - 1 - (j * pages + g))], 0, 0)
        return index

    page_specs = [pl.BlockSpec((None, None, page_size, width), page_map(g)) for g in range(pages)]
    grid_spec = pltpu.PrefetchScalarGridSpec(
        num_scalar_prefetch=1,
        grid=(n_seq, steps),
        in_specs=[
            pl.BlockSpec((None, width, cols), lambda n, j, _: (n, 0, 0)),
            pl.BlockSpec((1, cols), lambda n, j, _: (0, 0)),
            pl.BlockSpec((page_size, page_size), lambda n, j, _: (0, 0)),
            pl.BlockSpec((cols, width), lambda n, j, _: (0, 0)),
            pl.BlockSpec((None, page_size, width), lambda n, j, _: (n, 0, 0)),
            pl.BlockSpec((None, page_size, width), lambda n, j, _: (n, 0, 0)),
        ] + page_specs + page_specs,
        out_specs=pl.BlockSpec((None, dec_seq, width), lambda n, j, _: (n, 0, 0)),
        scratch_shapes=[pltpu.VMEM((1, cols), F32), pltpu.VMEM((cols, width), F32)],
    )
    return pl.pallas_call(
        functools.partial(_attn_sample_kernel, pages=pages, dec_seq=dec_seq, scale=head_dim ** -0.5),
        out_shape=jax.ShapeDtypeStruct((n_seq, dec_seq, width), F32),
        grid_spec=grid_spec,
        compiler_params=_params("parallel", "arbitrary"),
        name="attn_sample",
    )(page_table.reshape(-1), qbd, bias_row, lmat, sel, knew, vnew, *([cache_k] * pages), *([cache_v] * pages))


def _ab_out_kernel(x_ref, att_ref, u_ref, prev_ref, wp_ref, ps_ref, woa_ref, wop_ref, o_ref, ext_ref,
                   *, att_transposed, halo_from_u, tiles_per_seq, first_pos):
    i = pl.program_id(0)
    nseq, rows, width = u_ref.shape
    gw = width // len(POOL_WINDOWS)
    u = u_ref[...]
    prev = prev_ref[...]
    if halo_from_u:
        prev = jnp.where(i % tiles_per_seq == 0, jnp.zeros_like(prev), prev)
        pos0 = (i % tiles_per_seq) * rows
    else:
        pos0 = 0
    ext_ref[:, 0:POOL_HALO, :] = prev
    ext_ref[:, POOL_HALO:, :] = u
    pos = first_pos + pos0 + lax.broadcasted_iota(jnp.int32, (1, rows, 1), 1)
    pooled = []
    for g, w in enumerate(POOL_WINDOWS):
        lo = g * gw
        wsum = u[:, :, lo:lo + gw]
        for back in range(1, w):
            wsum = wsum + ext_ref[:, POOL_HALO - back:POOL_HALO - back + rows, lo:lo + gw]
        cnt = jnp.minimum(pos + 1, w).astype(F32)
        d = (wsum / cnt - u[:, :, lo:lo + gw]).reshape(nseq * rows, gw).astype(BF16)
        pooled.append(jnp.dot(d, wp_ref[g], preferred_element_type=F32))
    pool = (jnp.concatenate(pooled, axis=-1) * ps_ref[...]).astype(BF16)
    if att_transposed:
        att = att_ref[...].T.astype(BF16)
    else:
        att = att_ref[...].astype(BF16)
    y = (jnp.dot(att, woa_ref[...], preferred_element_type=F32)
         + jnp.dot(pool, wop_ref[...], preferred_element_type=F32))
    o_ref[...] = x_ref[...] + y


def _ab_out(x, att, u3, prev3, wp, ps, woa, wop, *, att_transposed, halo_from_u, tiles_per_seq, first_pos):
    t, d = x.shape
    n_all, rows, width = u3.shape
    if halo_from_u:
        nseq_tile, grid = 1, n_all
        prev_spec = pl.BlockSpec((1, POOL_HALO, width),
                                 lambda i: (jnp.maximum(i * (rows // POOL_HALO) - 1, 0), 0, 0))
    else:
        nseq_tile, grid = n_all, 1
        prev_spec = pl.BlockSpec((n_all, POOL_HALO, width), lambda i: (0, 0, 0))
    tm = nseq_tile * rows
    if att_transposed:
        att_spec = pl.BlockSpec((width, tm), lambda i: (0, i))
    else:
        att_spec = pl.BlockSpec((tm, width), lambda i: (i, 0))
    return pl.pallas_call(
        functools.partial(_ab_out_kernel, att_transposed=att_transposed, halo_from_u=halo_from_u,
                          tiles_per_seq=tiles_per_seq, first_pos=first_pos),
        out_shape=jax.ShapeDtypeStruct((t, d), F32),
        grid=(grid,),
        in_specs=[
            pl.BlockSpec((tm, d), lambda i: (i, 0)),
            att_spec,
            pl.BlockSpec((nseq_tile, rows, width), lambda i: (i, 0, 0)),
            prev_spec,
            _resident(wp.shape),
            _resident((1, width)),
            _resident(woa.shape),
            _resident(wop.shape),
        ],
        out_specs=pl.BlockSpec((tm, d), lambda i: (i, 0)),
        scratch_shapes=[pltpu.VMEM((nseq_tile, POOL_HALO + rows, width), F32)],
        compiler_params=_params("parallel"),
        name="ab_out",
    )(x, att, u3, prev3, wp, ps, woa, wop)


def _gelu(x):
    return 0.5 * x * (1.0 + jnp.tanh(GELU_C * (x + 0.044715 * (x * x * x))))


def _c_mixer_kernel(x_ref, g_ref, wu_ref, wv_ref, lng_ref, lnb_ref, ws_ref, bs_ref, wo_ref, *rest, emit_vn):
    if emit_vn:
        o_ref, vn_ref, gate_ref = rest
    else:
        o_ref, gate_ref = rest
    x = x_ref[...]
    h = _rms(x, g_ref[...]).astype(BF16)
    zu = _gelu(jnp.dot(h, wu_ref[...], preferred_element_type=F32))
    zv = _gelu(jnp.dot(h, wv_ref[...], preferred_element_type=F32))
    mu = jnp.mean(zv, axis=-1, keepdims=True)
    zc = zv - mu
    vn = zc * lax.rsqrt(jnp.mean(zc * zc, axis=-1, keepdims=True) + EPS) * lng_ref[...] + lnb_ref[...]
    if emit_vn:
        vn_ref[...] = vn
    vnb = vn.astype(BF16)
    groups, chunk, _ = ws_ref.shape
    gw = vn.shape[1] // groups
    for c in range(vn.shape[0] // chunk):
        r0 = c * chunk
        for g in range(groups):
            gate_ref[r0:r0 + chunk, g * gw:(g + 1) * gw] = jnp.dot(
                ws_ref[g], vnb[r0:r0 + chunk, g * gw:(g + 1) * gw], preferred_element_type=F32)
        gate_ref[r0:r0 + chunk, :] += bs_ref[...]
    mixed = (zu * gate_ref[...]).astype(BF16)
    o_ref[...] = x + jnp.dot(mixed, wo_ref[...], preferred_element_type=F32)


def _c_mixer(x, g, wu, wv, lng, lnb, ws, bs, wo, tm, emit_vn):
    t, d = x.shape
    cw = wu.shape[1]
    row = lambda i: (i, 0)
    out_shape = [jax.ShapeDtypeStruct((t, d), F32)]
    out_specs = [pl.BlockSpec((tm, d), row)]
    if emit_vn:
        out_shape.append(jax.ShapeDtypeStruct((t, cw), F32))
        out_specs.append(pl.BlockSpec((tm, cw), row))
    res = pl.pallas_call(
        functools.partial(_c_mixer_kernel, emit_vn=emit_vn),
        out_shape=tuple(out_shape),
        grid=(t // tm,),
        in_specs=[
            pl.BlockSpec((tm, d), row),
            _resident((1, d)),
            _resident(wu.shape),
            _resident(wv.shape),
            _resident((1, cw)),
            _resident((1, cw)),
            _resident(ws.shape),
            _resident(bs.shape),
            _resident(wo.shape),
        ],
        out_specs=tuple(out_specs),
        scratch_shapes=[pltpu.VMEM((tm, cw), F32)],
        compiler_params=_params("parallel"),
        name="c_mixer",
    )(x, g, wu, wv, lng, lnb, ws, bs, wo)
    return res if emit_vn else res[0]


def _suffix_matrix(n):
    r = jnp.arange(n)
    return (r[None, :] > r[:, None]).astype(BF16)


def kernel(x_prompt, x_sample, cache_k, cache_v, state_pool, page_table, norm_ffn1, norm_mix, norm_ffn2, ffn1_gate, ffn1_up, ffn1_down, ffn2_gate, ffn2_up, ffn2_down, ab_w_in, ab_q_norm, ab_k_norm, ab_sb_bias, ab_w_pool, ab_pool_scale, ab_w_out, c_w_in, c_ln_g, c_ln_b, c_w_spatial, c_b_spatial, c_w_out):
    batch, seq, d_model = x_prompt.shape
    dec_b, dec_seq, _ = x_sample.shape
    n_ab, n_phys, page_size, heads, head_dim = cache_k.shape
    width = heads * head_dim
    pool_width = ab_pool_scale.shape[1]
    past_len = page_table.shape[1] * page_size
    depth = norm_ffn1.shape[0]
    tp = batch * seq
    ts = dec_b * dec_seq
    tm_p = 512
    tk, tq = ATTN_TK, ATTN_TQ
    assert width == pool_width and seq % tm_p == 0 and tm_p % tk == 0 and ts % tk == 0 and dec_seq < POOL_HALO

    xp = x_prompt.reshape(tp, d_model)
    xs = x_sample.reshape(ts, d_model)
    ck = cache_k.reshape(n_ab, n_phys, page_size, width)
    cv = cache_v.reshape(n_ab, n_phys, page_size, width)
    bf = lambda w: w.astype(BF16)
    row = lambda v: v.reshape(1, -1).astype(F32)

    kp_l, vp_l, pp_l, ks_l, vs_l, ps_l, cv_l = [], [], [], [], [], [], []
    for layer in range(depth):
        f1 = (row(norm_ffn1[layer]), bf(ffn1_gate[layer]), bf(ffn1_up[layer]), bf(ffn1_down[layer]))
        f2 = (row(norm_ffn2[layer]), bf(ffn2_gate[layer]), bf(ffn2_up[layer]), bf(ffn2_down[layer]))
        xp = _ffn(xp, *f1, tm=tm_p)
        xs = _ffn(xs, *f1, tm=ts)
        gmix = row(norm_mix[layer])
        if layer % 2 == 0:
            a = layer // 2
            w_in = ab_w_in[a]
            wn = bf(w_in[:, width:])
            wt = bf(jnp.concatenate([w_in[:, :width], w_in[:, 2 * width:3 * width]], axis=1).T)
            hsum = jnp.kron(jnp.eye(heads, dtype=F32), jnp.ones((head_dim, head_dim), F32)).astype(BF16)
            gk = row(jnp.tile(ab_k_norm[a], heads))
            gq = jnp.tile(ab_q_norm[a], heads).reshape(width, 1).astype(F32)
            bias = ab_sb_bias[a].astype(F32)
            wp = bf(ab_w_pool[a])
            ps = row(ab_pool_scale[a])
            woa = bf(ab_w_out[a][:width])
            wop = bf(ab_w_out[a][width:])

            k_p, v_p, u_p, kb_p, qt_p, vt_p = _ab_project(xp, gmix, wn, wt, hsum, gk, gq, head_dim, tm_p, tk)
            att_p = _attn_prompt(bias, qt_p, kb_p, vt_p, _suffix_matrix(tk), batch, seq, head_dim, tq, tk)
            u3 = u_p.reshape(tp // tm_p, tm_p, pool_width)
            halo = u_p.reshape(tp // POOL_HALO, POOL_HALO, pool_width)
            xp = _ab_out(xp, att_p, u3, halo, wp, ps, woa, wop, att_transposed=True, halo_from_u=True,
                         tiles_per_seq=seq // tm_p, first_pos=0)
            kp_l.append(k_p.reshape(batch, seq, heads, head_dim))
            vp_l.append(v_p.reshape(batch, seq, heads, head_dim))
            pp_l.append(u_p.reshape(batch, seq, pool_width)[:, seq - (POOL_HALO - 1):])

            k_s, v_s, u_s, _, qt_s, _ = _ab_project(xs, gmix, wn, wt, hsum, gk, gq, head_dim, ts, tk)
            q4 = qt_s.reshape(heads, head_dim, dec_b, dec_seq).transpose(2, 0, 1, 3)
            qbd = (q4[:, :, :, None, :] * jnp.eye(heads, dtype=BF16)[None, :, None, :, None]).reshape(
                dec_b, width, heads * dec_seq)
            bias_row = jnp.repeat(bias, dec_seq).reshape(1, heads * dec_seq)
            sel = jnp.kron(jnp.eye(heads, dtype=F32), jnp.ones((dec_seq, head_dim), F32))
            pad = ((0, 0), (0, page_size - dec_seq), (0, 0))
            knew = jnp.pad(k_s.reshape(dec_b, dec_seq, width), pad)
            vnew = jnp.pad(v_s.reshape(dec_b, dec_seq, width), pad)
            att_s = _attn_sample(page_table, qbd, bias_row, _suffix_matrix(page_size), sel, knew, vnew, ck, cv,
                                 a, dec_seq, head_dim, SAMPLE_PAGES_PER_STEP)
            prev = jnp.pad(state_pool[a], ((0, 0), (1, 0), (0, 0)))
            us3 = u_s.reshape(dec_b, dec_seq, pool_width)
            xs = _ab_out(xs, att_s.reshape(ts, width), us3, prev, wp, ps, woa, wop, att_transposed=False,
                         halo_from_u=False, tiles_per_seq=1, first_pos=past_len)
            ks_l.append(k_s.reshape(dec_b, dec_seq, heads, head_dim))
            vs_l.append(v_s.reshape(dec_b, dec_seq, heads, head_dim))
            ps_l.append(jnp.concatenate([state_pool[a], us3], axis=1)[:, dec_seq:])
        else:
            c = layer // 2
            cw = c_ln_g.shape[1]
            groups, chunk, _ = c_w_spatial[c].shape
            wu = bf(c_w_in[c][:, :cw])
            wv = bf(c_w_in[c][:, cw:])
            lng, lnb = row(c_ln_g[c]), row(c_ln_b[c])
            wo = bf(c_w_out[c])
            causal = jnp.tril(jnp.ones((chunk, chunk), dtype=bool))
            ws = jnp.where(causal, c_w_spatial[c], 0)
            bs = jnp.repeat(c_b_spatial[c].T, cw // groups, axis=1).astype(F32)
            xp = _c_mixer(xp, gmix, wu, wv, lng, lnb, bf(ws), bs, wo, tm_p, emit_vn=False)
            eye = jnp.eye(dec_b, dtype=F32)
            ws_s = jax.vmap(lambda m: jnp.kron(eye, m))(ws[:, :dec_seq, :dec_seq])
            bs_s = jnp.tile(bs[:dec_seq], (dec_b, 1))
            xs, vn_s = _c_mixer(xs, gmix, wu, wv, lng, lnb, bf(ws_s), bs_s, wo, ts, emit_vn=True)
            cv_l.append(vn_s.reshape(dec_b, dec_seq, cw))
        xp = _ffn(xp, *f2, tm=tm_p)
        xs = _ffn(xs, *f2, tm=ts)

    return (xp.reshape(batch, seq, d_model), xs.reshape(dec_b, dec_seq, d_model),
            jnp.stack(kp_l), jnp.stack(vp_l), jnp.stack(pp_l),
            jnp.stack(ks_l), jnp.stack(vs_l), jnp.stack(ps_l), jnp.stack(cv_l))
```

```python
import functools
import math

import jax
import jax.numpy as jnp
from jax import lax
from jax.experimental import pallas as pl
from jax.experimental.pallas import tpu as pltpu

F32 = jnp.float32
BF16 = jnp.bfloat16

EPS = 1e-6
LOG2E = math.log2(math.e)
POOL_WINDOWS = (2, 4, 8, 16)
POOL_HALO = 16
GELU_C = math.sqrt(2.0 / math.pi)

V7X_VMEM_LIMIT_BYTES = 56 * 1024 * 1024
V7X_LANES = 128
ATTN_TK = 256
ATTN_TQ = 256
ATTN_HEADS_PER_STEP = 8
ATTN_STAGE_LAG = 2
SAMPLE_PAGES_PER_STEP = 16


def _params(*sem):
    return pltpu.CompilerParams(dimension_semantics=sem, vmem_limit_bytes=V7X_VMEM_LIMIT_BYTES)


def _resident(shape):
    nd = len(shape)
    return pl.BlockSpec(shape, lambda *_: (0,) * nd, pipeline_mode=pl.Buffered(1))


def _rms(x, g):
    return x * lax.rsqrt(jnp.mean(x * x, axis=-1, keepdims=True) + EPS) * g


def _softplus2_parts(z):
    sp = jnp.maximum(z, 0.0) + jnp.log2(1.0 + jnp.exp2(-jnp.abs(z)))
    return sp, z - sp


def _ffn_kernel(x_ref, g_ref, wg_ref, wu_ref, wd_ref, o_ref):
    x = x_ref[...]
    h = _rms(x, g_ref[...]).astype(BF16)
    gate = jnp.dot(h, wg_ref[...], preferred_element_type=F32)
    up = jnp.dot(h, wu_ref[...], preferred_element_type=F32)
    act = (gate * jax.nn.sigmoid(gate) * up).astype(BF16)
    o_ref[...] = x + 0.5 * jnp.dot(act, wd_ref[...], preferred_element_type=F32)


def _ffn(x, g, wg, wu, wd, tm):
    t, d = x.shape
    dff = wg.shape[1]
    return pl.pallas_call(
        _ffn_kernel,
        out_shape=jax.ShapeDtypeStruct((t, d), F32),
        grid=(t // tm,),
        in_specs=[
            pl.BlockSpec((tm, d), lambda i: (i, 0)),
            _resident((1, d)),
            _resident((d, dff)),
            _resident((d, dff)),
            _resident((dff, d)),
        ],
        out_specs=pl.BlockSpec((tm, d), lambda i: (i, 0)),
        compiler_params=_params("parallel"),
        name="ffn",
    )(x, g, wg, wu, wd)


def _ab_project_kernel(x_ref, g_ref, wn_ref, wt_ref, hsum_ref, gk_ref, gqc_ref, gkc_ref, *outs,
                       width, head_dim, tk, natural_kv):
    if natural_kv:
        kb_ref, u_ref, qt_ref, kt_ref, vt_ref, vt3_ref, k_ref, v_ref = outs
    else:
        kb_ref, u_ref, qt_ref, kt_ref, vt_ref, vt3_ref = outs
    h = _rms(x_ref[...], g_ref[...]).astype(BF16)
    nat = jnp.dot(h, wn_ref[...], preferred_element_type=F32)
    k = nat[:, :width]
    ksq = k * k
    ksq_hi = ksq.astype(BF16)
    ksq_lo = (ksq - ksq_hi.astype(F32)).astype(BF16)
    ss = (jnp.dot(ksq_hi, hsum_ref[...], preferred_element_type=F32)
          + jnp.dot(ksq_lo, hsum_ref[...], preferred_element_type=F32))
    kn = k * lax.rsqrt(ss * (1.0 / head_dim) + EPS) * gk_ref[...]
    kb_ref[...] = kn.astype(BF16)
    if natural_kv:
        k_ref[...] = kn
        v_ref[...] = nat[:, width:2 * width]
    u_ref[...] = nat[:, nat.shape[1] - width:]
    tr = lax.dot_general(wt_ref[...], h, (((1,), (1,)), ((), ())), preferred_element_type=F32)
    gqc = gqc_ref[...]
    gkc = gkc_ref[...]
    for hd in range(width // head_dim):
        r0 = hd * head_dim
        q = tr[r0:r0 + head_dim, :]
        qn = q * lax.rsqrt(jnp.mean(q * q, axis=0, keepdims=True) + EPS)
        qt_ref[r0:r0 + head_dim, :] = (qn * gqc[r0:r0 + head_dim, :]).astype(BF16)
        kk = tr[width + r0:width + r0 + head_dim, :]
        kkn = kk * lax.rsqrt(jnp.mean(kk * kk, axis=0, keepdims=True) + EPS)
        kt_ref[r0:r0 + head_dim, :] = kkn * gkc[r0:r0 + head_dim, :]
    vt = tr[2 * width:, :]
    vt_ref[...] = vt
    vtb = vt.astype(BF16)
    for c in range(vt.shape[1] // tk):
        vt3_ref[c] = vtb[:, c * tk:(c + 1) * tk]


def _ab_project(x, g, wn, wt, hsum, gk, gqc, gkc, head_dim, seq, tm, tk, natural_kv):
    t, d = x.shape
    width = gk.shape[1]
    tiles_per_seq = seq // tm
    row = lambda i: (i, 0)
    col = lambda i: (0, i)
    seq_col = lambda i: (i // tiles_per_seq, 0, i % tiles_per_seq)
    out_shape = [
        jax.ShapeDtypeStruct((t, width), BF16),
        jax.ShapeDtypeStruct((t, width), F32),
        jax.ShapeDtypeStruct((width, t), BF16),
        jax.ShapeDtypeStruct((t // seq, width, seq), F32),
        jax.ShapeDtypeStruct((t // seq, width, seq), F32),
        jax.ShapeDtypeStruct((t // tk, width, tk), BF16),
    ]
    out_specs = [
        pl.BlockSpec((tm, width), row),
        pl.BlockSpec((tm, width), row),
        pl.BlockSpec((width, tm), col),
        pl.BlockSpec((None, width, tm), seq_col),
        pl.BlockSpec((None, width, tm), seq_col),
        pl.BlockSpec((tm // tk, width, tk), lambda i: (i, 0, 0)),
    ]
    if natural_kv:
        out_shape += [jax.ShapeDtypeStruct((t, width), F32)] * 2
        out_specs += [pl.BlockSpec((tm, width), row)] * 2
    return pl.pallas_call(
        functools.partial(_ab_project_kernel, width=width, head_dim=head_dim, tk=tk, natural_kv=natural_kv),
        out_shape=tuple(out_shape),
        grid=(t // tm,),
        in_specs=[
            pl.BlockSpec((tm, d), row),
            _resident((1, d)),
            _resident(wn.shape),
            _resident(wt.shape),
            _resident(hsum.shape),
            _resident((1, width)),
            _resident((width, 1)),
            _resident((width, 1)),
        ],
        out_specs=tuple(out_specs),
        compiler_params=_params("parallel"),
        name="ab_project",
    )(x, g, wn, wt, hsum, gk, gqc, gkc)


def _attn_prompt_kernel(bias_ref, qt_ref, k_ref, vt_ref, lmat_ref, o_ref, *, tq, tk, head_dim, heads_per_step):
    hg = pl.program_id(1)
    qi = pl.program_id(2)
    pair = 2 * head_dim
    lmat = lmat_ref[...]
    ntile = tq // tk

    row = lax.broadcasted_iota(jnp.int32, (pair, tq), 0)
    qms, biases = [], []
    for hl in range(heads_per_step):
        p0 = (hl // 2) * pair
        q = qt_ref[p0:p0 + pair, :]
        own = (hl % 2) * head_dim
        qms.append(jnp.where((row >= own) & (row < own + head_dim), q, jnp.zeros_like(q)))
        biases.append(bias_ref[hg * heads_per_step + hl] * LOG2E)

    def tile(j, carries, masked):
        k0 = pl.multiple_of(j * tk, tk)
        if masked:
            kpos = k0 + lax.broadcasted_iota(jnp.int32, (tk, tq), 0)
            qpos = qi * tq + lax.broadcasted_iota(jnp.int32, (tk, tq), 1)
            valid = kpos < qpos
        scores, lsigs, suffixes, firsts, pvs = {}, {}, {}, {}, {}
        new = list(carries)
        for n in range(heads_per_step + 3 * ATTN_STAGE_LAG):
            hl = n
            if hl < heads_per_step:
                p0 = (hl // 2) * pair
                scores[hl] = jnp.dot(k_ref[pl.ds(k0, tk), p0:p0 + pair], qms[hl],
                                     preferred_element_type=F32)
            hl = n - ATTN_STAGE_LAG
            if 0 <= hl < heads_per_step:
                sp, lsigs[hl] = _softplus2_parts(scores.pop(hl) + biases[hl])
                if masked:
                    sp = jnp.where(valid, sp, 0.0)
                spb = sp.astype(BF16)
                firsts[hl] = spb[0:1, :].astype(F32)
                suffixes[hl] = jnp.dot(lmat, spb, preferred_element_type=F32)
            hl = n - 2 * ATTN_STAGE_LAG
            if 0 <= hl < heads_per_step:
                suffix = suffixes.pop(hl)
                a = jnp.exp2(lsigs.pop(hl) - suffix)
                if masked:
                    a = jnp.where(valid, a, 0.0)
                r0 = hl * head_dim
                pvs[hl] = jnp.dot(vt_ref[j, r0:r0 + head_dim, :], a.astype(BF16),
                                  preferred_element_type=F32)
                new[hl] = carries[hl] + suffix[0:1, :] + firsts.pop(hl)
            hl = n - 3 * ATTN_STAGE_LAG
            if 0 <= hl < heads_per_step:
                r0 = hl * head_dim
                o_ref[r0:r0 + head_dim, :] += pvs.pop(hl) * jnp.exp2(-carries[hl])
        return tuple(new)

    o_ref[...] = jnp.zeros_like(o_ref)
    carries = tuple(jnp.zeros((1, tq), F32) for _ in range(heads_per_step))
    for d in range(ntile):
        carries = tile(qi * ntile + (ntile - 1 - d), carries, True)
    lax.fori_loop(0, qi * ntile, lambda it, c: tile(qi * ntile - 1 - it, c, False), carries)


def _attn_prompt(bias, qt, kb, vt3, lmat, batch, seq, head_dim, tq, tk, heads_per_step):
    width = qt.shape[0]
    nq = seq // tq
    gw = heads_per_step * head_dim
    grid_spec = pltpu.PrefetchScalarGridSpec(
        num_scalar_prefetch=1,
        grid=(batch, width // gw, nq),
        in_specs=[
            pl.BlockSpec((gw, tq), lambda b, g, i, _: (g, b * nq + i)),
            pl.BlockSpec((seq, gw), lambda b, g, i, _: (b, g), pipeline_mode=pl.Buffered(1)),
            pl.BlockSpec((seq // tk, gw, tk), lambda b, g, i, _: (b, g, 0), pipeline_mode=pl.Buffered(1)),
            pl.BlockSpec((tk, tk), lambda b, g, i, _: (0, 0), pipeline_mode=pl.Buffered(1)),
        ],
        out_specs=pl.BlockSpec((gw, tq), lambda b, g, i, _: (g, b * nq + i)),
    )
    return pl.pallas_call(
        functools.partial(_attn_prompt_kernel, tq=tq, tk=tk, head_dim=head_dim, heads_per_step=heads_per_step),
        out_shape=jax.ShapeDtypeStruct((width, batch * seq), F32),
        grid_spec=grid_spec,
        compiler_params=_params("parallel", "parallel", "arbitrary"),
        name="attn_prompt",
    )(bias, qt, kb, vt3, lmat)


def _attn_sample_kernel(pt_ref, qbd_ref, bias_ref, umat_ref, sel_ref, knew_ref, vnew_ref, *rest, pages, dec_seq):
    k_refs = rest[:pages]
    v_refs = rest[pages:2 * pages]
    o_ref = rest[2 * pages]
    carry_ref, acc_ref = rest[2 * pages + 1:]
    j = pl.program_id(1)
    qbd = qbd_ref[...]
    bias = bias_ref[...]

    def tiles(loaders, masked):
        count = len(loaders)
        scores, lsigs, suffixes, totals, pvs, valids = {}, {}, {}, {}, {}, {}
        carry = carry_ref[...]
        for step in range(count + 3 * ATTN_STAGE_LAG):
            m = step
            if m < count:
                scores[m] = jnp.dot(qbd, loaders[m][0]().astype(BF16), preferred_element_type=F32)
            m = step - ATTN_STAGE_LAG
            if 0 <= m < count:
                z = scores.pop(m) + bias
                n = z.shape[1]
                sp, lsigs[m] = _softplus2_parts(z)
                if masked:
                    key = lax.broadcasted_iota(jnp.int32, z.shape, 1)
                    qry = lax.broadcasted_iota(jnp.int32, z.shape, 0) % dec_seq
                    valids[m] = key < qry
                    sp = jnp.where(valids[m], sp, 0.0)
                spb = sp.astype(BF16)
                suffixes[m] = jnp.dot(spb, umat_ref[0:n, 0:n], preferred_element_type=F32)
                totals[m] = spb[:, 0:1].astype(F32)
            m = step - 2 * ATTN_STAGE_LAG
            if 0 <= m < count:
                suffix = suffixes.pop(m)
                a = jnp.exp2(lsigs.pop(m) - suffix)
                if masked:
                    a = jnp.where(valids.pop(m), a, 0.0)
                pvs[m] = lax.dot_general(a.astype(BF16), loaders[m][1]().astype(BF16), (((1,), (1,)), ((), ())),
                                         preferred_element_type=F32)
                totals[m] = totals[m] + suffix[:, 0:1]
            m = step - 3 * ATTN_STAGE_LAG
            if 0 <= m < count:
                acc_ref[...] += pvs.pop(m) * jnp.exp2(-carry)
                carry = carry + totals.pop(m)
        carry_ref[...] = carry

    @pl.when(j == 0)
    def _():
        carry_ref[...] = jnp.zeros_like(carry_ref)
        acc_ref[...] = jnp.zeros_like(acc_ref)
        tiles([(lambda: knew_ref[...], lambda: vnew_ref[...])], True)

    def side_by_side(refs, g):
        return lambda: jnp.concatenate([refs[g + 1][...], refs[g][...]], axis=1)

    tiles([(side_by_side(k_refs, g), side_by_side(v_refs, g)) for g in range(0, pages, 2)], False)

    @pl.when(j == pl.num_programs(1) - 1)
    def _():
        own = acc_ref[...] * sel_ref[...]
        heads = own.shape[0] // dec_seq
        out = own[0:dec_seq, :]
        for hd in range(1, heads):
            out = out + own[hd * dec_seq:(hd + 1) * dec_seq, :]
        o_ref[...] = out


def _attn_sample(page_table, qbd, bias_col, umat, sel, knew, vnew, cache_kt, cache_vt, layer, dec_seq, pages):
    n_seq, n_pages = page_table.shape
    _, _, width, page_size = cache_kt.shape
    rows = qbd.shape[1]
    steps = n_pages // pages

    def page_map(g):
        def index(n, j, pt):
            return (layer, pt[n * n_pages + (n_pages - 1 - (j * pages + g))], 0, 0)
        return index

    page_specs = [pl.BlockSpec((None, None, width, page_size), page_map(g)) for g in range(pages)]
    const = lambda n, j, _: (0, 0)
    per_seq = lambda n, j, _: (n, 0, 0)
    grid_spec = pltpu.PrefetchScalarGridSpec(
        num_scalar_prefetch=1,
        grid=(n_seq, steps),
        in_specs=[
            pl.BlockSpec((None, rows, width), per_seq),
            pl.BlockSpec((rows, 1), const),
            pl.BlockSpec(umat.shape, const),
            pl.BlockSpec((rows, width), const),
            pl.BlockSpec((None, width, page_size), per_seq),
            pl.BlockSpec((None, width, page_size), per_seq),
        ] + page_specs + page_specs,
        out_specs=pl.BlockSpec((None, dec_seq, width), per_seq),
        scratch_shapes=[pltpu.VMEM((rows, 1), F32), pltpu.VMEM((rows, width), F32)],
    )
    return pl.pallas_call(
        functools.partial(_attn_sample_kernel, pages=pages, dec_seq=dec_seq),
        out_shape=jax.ShapeDtypeStruct((n_seq, dec_seq, width), F32),
        grid_spec=grid_spec,
        compiler_params=_params("parallel", "arbitrary"),
        name="attn_sample",
    )(page_table.reshape(-1), qbd, bias_col, umat, sel, knew, vnew, *([cache_kt] * pages), *([cache_vt] * pages))


def _ab_out_kernel(x_ref, att_ref, u_ref, prev_ref, wp_ref, ps_ref, woa_ref, wop_ref, o_ref, ext_ref,
                   *, att_transposed, halo_from_u, tiles_per_seq, first_pos):
    i = pl.program_id(0)
    nseq, rows, width = u_ref.shape
    gw = width // len(POOL_WINDOWS)
    u = u_ref[...]
    prev = prev_ref[...]
    if halo_from_u:
        prev = jnp.where(i % tiles_per_seq == 0, jnp.zeros_like(prev), prev)
        pos0 = (i % tiles_per_seq) * rows
    else:
        pos0 = 0
    ext_ref[:, 0:POOL_HALO, :] = prev
    ext_ref[:, POOL_HALO:, :] = u
    pos = first_pos + pos0 + lax.broadcasted_iota(jnp.int32, (1, rows, 1), 1)
    pooled = []
    for g, w in enumerate(POOL_WINDOWS):
        lo = g * gw
        wsum = u[:, :, lo:lo + gw]
        for back in range(1, w):
            wsum = wsum + ext_ref[:, POOL_HALO - back:POOL_HALO - back + rows, lo:lo + gw]
        cnt = jnp.minimum(pos + 1, w).astype(F32)
        d = (wsum / cnt - u[:, :, lo:lo + gw]).reshape(nseq * rows, gw).astype(BF16)
        pooled.append(jnp.dot(d, wp_ref[g], preferred_element_type=F32))
    pool = (jnp.concatenate(pooled, axis=-1) * ps_ref[...]).astype(BF16)
    if att_transposed:
        att = att_ref[...].T.astype(BF16)
    else:
        att = att_ref[...].astype(BF16)
    y = (jnp.dot(att, woa_ref[...], preferred_element_type=F32)
         + jnp.dot(pool, wop_ref[...], preferred_element_type=F32))
    o_ref[...] = x_ref[...] + y


def _ab_out(x, att, u3, prev3, wp, ps, woa, wop, *, att_transposed, halo_from_u, tiles_per_seq, first_pos):
    t, d = x.shape
    n_all, rows, width = u3.shape
    if halo_from_u:
        nseq_tile, grid = 1, n_all
        prev_spec = pl.BlockSpec((1, POOL_HALO, width),
                                 lambda i: (jnp.maximum(i * (rows // POOL_HALO) - 1, 0), 0, 0))
    else:
        nseq_tile, grid = n_all, 1
        prev_spec = pl.BlockSpec((n_all, POOL_HALO, width), lambda i: (0, 0, 0))
    tm = nseq_tile * rows
    if att_transposed:
        att_spec = pl.BlockSpec((width, tm), lambda i: (0, i))
    else:
        att_spec = pl.BlockSpec((tm, width), lambda i: (i, 0))
    return pl.pallas_call(
        functools.partial(_ab_out_kernel, att_transposed=att_transposed, halo_from_u=halo_from_u,
                          tiles_per_seq=tiles_per_seq, first_pos=first_pos),
        out_shape=jax.ShapeDtypeStruct((t, d), F32),
        grid=(grid,),
        in_specs=[
            pl.BlockSpec((tm, d), lambda i: (i, 0)),
            att_spec,
            pl.BlockSpec((nseq_tile, rows, width), lambda i: (i, 0, 0)),
            prev_spec,
            _resident(wp.shape),
            _resident((1, width)),
            _resident(woa.shape),
            _resident(wop.shape),
        ],
        out_specs=pl.BlockSpec((tm, d), lambda i: (i, 0)),
        scratch_shapes=[pltpu.VMEM((nseq_tile, POOL_HALO + rows, width), F32)],
        compiler_params=_params("parallel"),
        name="ab_out",
    )(x, att, u3, prev3, wp, ps, woa, wop)


def _gelu(x):
    return 0.5 * x * (1.0 + jnp.tanh(GELU_C * (x + 0.044715 * (x * x * x))))


def _c_mixer_kernel(x_ref, g_ref, wu_ref, wv_ref, lng_ref, lnb_ref, ws_ref, bs_ref, wo_ref, *rest, emit_vn):
    if emit_vn:
        o_ref, vn_ref, gate_ref = rest
    else:
        o_ref, gate_ref = rest
    x = x_ref[...]
    h = _rms(x, g_ref[...]).astype(BF16)
    zu = _gelu(jnp.dot(h, wu_ref[...], preferred_element_type=F32))
    zv = _gelu(jnp.dot(h, wv_ref[...], preferred_element_type=F32))
    mu = jnp.mean(zv, axis=-1, keepdims=True)
    zc = zv - mu
    vn = zc * lax.rsqrt(jnp.mean(zc * zc, axis=-1, keepdims=True) + EPS) * lng_ref[...] + lnb_ref[...]
    if emit_vn:
        vn_ref[...] = vn
    vnb = vn.astype(BF16)
    groups, chunk, _ = ws_ref.shape
    gw = vn.shape[1] // groups
    for c in range(vn.shape[0] // chunk):
        r0 = c * chunk
        for g in range(groups):
            gate_ref[r0:r0 + chunk, g * gw:(g + 1) * gw] = jnp.dot(
                ws_ref[g], vnb[r0:r0 + chunk, g * gw:(g + 1) * gw], preferred_element_type=F32)
        gate_ref[r0:r0 + chunk, :] += bs_ref[...]
    mixed = (zu * gate_ref[...]).astype(BF16)
    o_ref[...] = x + jnp.dot(mixed, wo_ref[...], preferred_element_type=F32)


def _c_mixer(x, g, wu, wv, lng, lnb, ws, bs, wo, tm, emit_vn):
    t, d = x.shape
    cw = wu.shape[1]
    row = lambda i: (i, 0)
    out_shape = [jax.ShapeDtypeStruct((t, d), F32)]
    out_specs = [pl.BlockSpec((tm, d), row)]
    if emit_vn:
        out_shape.append(jax.ShapeDtypeStruct((t, cw), F32))
        out_specs.append(pl.BlockSpec((tm, cw), row))
    res = pl.pallas_call(
        functools.partial(_c_mixer_kernel, emit_vn=emit_vn),
        out_shape=tuple(out_shape),
        grid=(t // tm,),
        in_specs=[
            pl.BlockSpec((tm, d), row),
            _resident((1, d)),
            _resident(wu.shape),
            _resident(wv.shape),
            _resident((1, cw)),
            _resident((1, cw)),
            _resident(ws.shape),
            _resident(bs.shape),
            _resident(wo.shape),
        ],
        out_specs=tuple(out_specs),
        scratch_shapes=[pltpu.VMEM((tm, cw), F32)],
        compiler_params=_params("parallel"),
        name="c_mixer",
    )(x, g, wu, wv, lng, lnb, ws, bs, wo)
    return res if emit_vn else res[0]


def _after_matrix(n):
    r = jnp.arange(n)
    return (r[:, None] > r[None, :]).astype(BF16)


def kernel(x_prompt, x_sample, cache_k, cache_v, state_pool, page_table, norm_ffn1, norm_mix, norm_ffn2, ffn1_gate, ffn1_up, ffn1_down, ffn2_gate, ffn2_up, ffn2_down, ab_w_in, ab_q_norm, ab_k_norm, ab_sb_bias, ab_w_pool, ab_pool_scale, ab_w_out, c_w_in, c_ln_g, c_ln_b, c_w_spatial, c_b_spatial, c_w_out):
    batch, seq, d_model = x_prompt.shape
    dec_b, dec_seq, _ = x_sample.shape
    n_ab, n_phys, page_size, heads, head_dim = cache_k.shape
    width = heads * head_dim
    pool_width = ab_pool_scale.shape[1]
    past_len = page_table.shape[1] * page_size
    depth = norm_ffn1.shape[0]
    tp = batch * seq
    ts = dec_b * dec_seq
    tm_p = 512
    tk, tq = ATTN_TK, ATTN_TQ
    assert width == pool_width and seq % tm_p == 0 and tm_p % tk == 0 and ts % tk == 0 and dec_seq < POOL_HALO
    assert page_size == V7X_LANES and 2 * page_size == tk

    xp = x_prompt.reshape(tp, d_model)
    xs = x_sample.reshape(ts, d_model)
    ckt = cache_k.transpose(0, 1, 3, 4, 2).reshape(n_ab, n_phys, width, page_size)
    cvt = cache_v.transpose(0, 1, 3, 4, 2).reshape(n_ab, n_phys, width, page_size)
    bf = lambda w: w.astype(BF16)
    row = lambda v: v.reshape(1, -1).astype(F32)
    colv = lambda v: v.reshape(-1, 1).astype(F32)

    kp_l, vp_l, pp_l, ks_l, vs_l, ps_l, cv_l = [], [], [], [], [], [], []
    for layer in range(depth):
        f1 = (row(norm_ffn1[layer]), bf(ffn1_gate[layer]), bf(ffn1_up[layer]), bf(ffn1_down[layer]))
        f2 = (row(norm_ffn2[layer]), bf(ffn2_gate[layer]), bf(ffn2_up[layer]), bf(ffn2_down[layer]))
        xp = _ffn(xp, *f1, tm=tm_p)
        xs = _ffn(xs, *f1, tm=ts)
        gmix = row(norm_mix[layer])
        if layer % 2 == 0:
            a = layer // 2
            w_in = ab_w_in[a]
            wq, wk, wv, wu = (w_in[:, i * width:(i + 1) * width] for i in range(4))
            wt = bf(jnp.concatenate([wq, wk, wv], axis=1).T)
            hsum = jnp.kron(jnp.eye(heads, dtype=F32), jnp.ones((head_dim, head_dim), F32)).astype(BF16)
            gk = row(jnp.tile(ab_k_norm[a], heads))
            gkc = colv(jnp.tile(ab_k_norm[a], heads))
            gqc = colv(jnp.tile(ab_q_norm[a], heads)) * (head_dim ** -0.5 * LOG2E)
            bias = ab_sb_bias[a].astype(F32)
            wp = bf(ab_w_pool[a])
            ps = row(ab_pool_scale[a])
            woa = bf(ab_w_out[a][:width])
            wop = bf(ab_w_out[a][width:])
            amat = _after_matrix(tk)

            kb_p, u_p, qt_p, kt_p, vt_p, vt3_p = _ab_project(
                xp, gmix, bf(jnp.concatenate([wk, wu], axis=1)), wt, hsum, gk, gqc, gkc, head_dim, seq, tm_p, tk, False)
            att_p = _attn_prompt(bias, qt_p, kb_p, vt3_p, amat.T, batch, seq, head_dim, tq, tk, ATTN_HEADS_PER_STEP)
            u3 = u_p.reshape(tp // tm_p, tm_p, pool_width)
            halo = u_p.reshape(tp // POOL_HALO, POOL_HALO, pool_width)
            xp = _ab_out(xp, att_p, u3, halo, wp, ps, woa, wop, att_transposed=True, halo_from_u=True,
                         tiles_per_seq=seq // tm_p, first_pos=0)
            to_cache = lambda m: m.reshape(batch, heads, head_dim, seq).transpose(0, 3, 1, 2)
            kp_l.append(to_cache(kt_p))
            vp_l.append(to_cache(vt_p))
            pp_l.append(u_p.reshape(batch, seq, pool_width)[:, seq - (POOL_HALO - 1):])

            _, u_s, qt_s, kt_s, vt_s, _, k_s, v_s = _ab_project(
                xs, gmix, bf(jnp.concatenate([wk, wv, wu], axis=1)), wt, hsum, gk, gqc, gkc, head_dim, ts, ts, tk, True)
            q4 = qt_s.reshape(heads, head_dim, dec_b, dec_seq).transpose(2, 0, 3, 1)
            qbd = (q4[:, :, :, None, :] * jnp.eye(heads, dtype=BF16)[None, :, None, :, None]).reshape(
                dec_b, heads * dec_seq, width)
            bias_col = colv(jnp.repeat(bias * LOG2E, dec_seq))
            sel = jnp.kron(jnp.eye(heads, dtype=F32), jnp.ones((dec_seq, head_dim), F32))
            as_page = lambda m: jnp.pad(m.reshape(width, dec_b, dec_seq).transpose(1, 0, 2),
                                        ((0, 0), (0, 0), (0, page_size - dec_seq)))
            att_s = _attn_sample(page_table, qbd, bias_col, amat, sel, as_page(kt_s), as_page(vt_s), ckt, cvt,
                                 a, dec_seq, SAMPLE_PAGES_PER_STEP)
            prev = jnp.pad(state_pool[a], ((0, 0), (1, 0), (0, 0)))
            us3 = u_s.reshape(dec_b, dec_seq, pool_width)
            xs = _ab_out(xs, att_s.reshape(ts, width), us3, prev, wp, ps, woa, wop, att_transposed=False,
                         halo_from_u=False, tiles_per_seq=1, first_pos=past_len)
            ks_l.append(k_s.reshape(dec_b, dec_seq, heads, head_dim))
            vs_l.append(v_s.reshape(dec_b, dec_seq, heads, head_dim))
            ps_l.append(jnp.concatenate([state_pool[a], us3], axis=1)[:, dec_seq:])
        else:
            c = layer // 2
            cw = c_ln_g.shape[1]
            groups, chunk, _ = c_w_spatial[c].shape
            wu = bf(c_w_in[c][:, :cw])
            wv = bf(c_w_in[c][:, cw:])
            lng, lnb = row(c_ln_g[c]), row(c_ln_b[c])
            wo = bf(c_w_out[c])
            causal = jnp.tril(jnp.ones((chunk, chunk), dtype=bool))
            ws = jnp.where(causal, c_w_spatial[c], 0)
            bs = jnp.repeat(c_b_spatial[c].T, cw // groups, axis=1).astype(F32)
            xp = _c_mixer(xp, gmix, wu, wv, lng, lnb, bf(ws), bs, wo, tm_p, emit_vn=False)
            eye = jnp.eye(dec_b, dtype=F32)
            ws_s = jax.vmap(lambda m: jnp.kron(eye, m))(ws[:, :dec_seq, :dec_seq])
            bs_s = jnp.tile(bs[:dec_seq], (dec_b, 1))
            xs, vn_s = _c_mixer(xs, gmix, wu, wv, lng, lnb, bf(ws_s), bs_s, wo, ts, emit_vn=True)
            cv_l.append(vn_s.reshape(dec_b, dec_seq, cw))
        xp = _ffn(xp, *f2, tm=tm_p)
        xs = _ffn(xs, *f2, tm=ts)

    return (xp.reshape(batch, seq, d_model), xs.reshape(dec_b, dec_seq, d_model),
            jnp.stack(kp_l), jnp.stack(vp_l), jnp.stack(pp_l),
            jnp.stack(ks_l), jnp.stack(vs_l), jnp.stack(ps_l), jnp.stack(cv_l))
```
